```python
import math
import jax, jax.numpy as jnp
from jax import lax
import numpy as np

D_MODEL = 1024
BATCH = 2
SEQ = 16384
DEPTH = 4

N_MIXERS = 2
SSM_GROUP = 16
SSM_GROUPS = D_MODEL // SSM_GROUP
SSM_STATE = 64
ATTN_HEADS = 8
ATTN_HD = D_MODEL // ATTN_HEADS // 2
Q_BLOCK = 128
REL_BUCKETS = 32
REL_MAX_EXACT = REL_BUCKETS // 2
REL_MAX_DIST = 128
D_FF = 2816
N_EXPERTS = 8
TOP_K = 2
RMS_EPS = 1e-6
N_SSM_LAYERS = (DEPTH + 1) // 2
N_ATTN_LAYERS = DEPTH // 2
N_DENSE_LAYERS = (DEPTH + 1) // 2
N_MOE_LAYERS = DEPTH // 2

kernel_name = "hybrid_s5_diffattn_moe_trunk"


def rms_norm(x, g):
    xf = x.astype(jnp.float32)
    y = xf * lax.rsqrt(jnp.mean(xf * xf, axis=-1, keepdims=True) + RMS_EPS)
    return (y * g.astype(jnp.float32)).astype(x.dtype)


def swiglu(h, w_gate_up, w_down):
    g, u = jnp.split(h @ w_gate_up, 2, axis=-1)
    return (jax.nn.silu(g) * u) @ w_down


def s5_mixer(u, a_re, a_im, log_dt, b_re, b_im, c_re, c_im, d_skip):
    f32 = jnp.float32
    L = u.shape[1]
    dt = jnp.exp(log_dt.astype(f32))[:, None]
    ar = a_re.astype(f32); ai = a_im.astype(f32)
    mag = jnp.exp(ar * dt)
    ab_re = mag * jnp.cos(ai * dt)
    ab_im = mag * jnp.sin(ai * dt)
    den = ar * ar + ai * ai
    nr = ab_re - 1.0; ni = ab_im
    coef_re = (nr * ar + ni * ai) / den
    coef_im = (ni * ar - nr * ai) / den
    br = b_re.astype(f32); bi = b_im.astype(f32)
    bb_re = coef_re[..., None] * br - coef_im[..., None] * bi
    bb_im = coef_re[..., None] * bi + coef_im[..., None] * br
    cr = c_re.astype(f32); ci = c_im.astype(f32)

    def combine(e1, e2):
        a1r, a1i, b1r, b1i = e1
        a2r, a2i, b2r, b2i = e2
        return (a2r * a1r - a2i * a1i,
                a2r * a1i + a2i * a1r,
                a2r * b1r - a2i * b1i + b2r,
                a2r * b1i + a2i * b1r + b2i)

    def per_sequence(ub):
        ug = ub.astype(f32).reshape(L, SSM_GROUPS, SSM_GROUP)
        x_re = jnp.einsum('lgc,gpc->lgp', ug, bb_re)
        x_im = jnp.einsum('lgc,gpc->lgp', ug, bb_im)
        a_re_l = jnp.broadcast_to(ab_re, x_re.shape)
        a_im_l = jnp.broadcast_to(ab_im, x_im.shape)
        _, _, s_re, s_im = lax.associative_scan(combine, (a_re_l, a_im_l, x_re, x_im), axis=0)
        y = jnp.einsum('lgp,gcp->lgc', s_re, cr) - jnp.einsum('lgp,gcp->lgc', s_im, ci)
        return y.reshape(L, D_MODEL)

    y = lax.map(per_sequence, u)
    y = y + d_skip.astype(f32) * u.astype(f32)
    return y.astype(u.dtype)


def rel_bucket(n):
    n = jnp.maximum(n, 0)
    is_small = n < REL_MAX_EXACT
    nf = jnp.maximum(n, 1).astype(jnp.float32)
    large = REL_MAX_EXACT + (jnp.log(nf / REL_MAX_EXACT) / math.log(REL_MAX_DIST / REL_MAX_EXACT)
                             * (REL_BUCKETS - REL_MAX_EXACT)).astype(jnp.int32)
    large = jnp.minimum(large, REL_BUCKETS - 1)
    return jnp.where(is_small, n, large)


def diff_attention(h, w_qkv, lq1, lk1, lq2, lk2, subln_g, w_o, rel_bias, lambda_init):
    B_, L, _ = h.shape
    q, k, v = jnp.split(h @ w_qkv, 3, axis=-1)
    q = q.reshape(B_, L, ATTN_HEADS, 2, ATTN_HD).transpose(0, 2, 3, 1, 4)
    k = k.reshape(B_, L, ATTN_HEADS, 2, ATTN_HD).transpose(0, 2, 3, 1, 4)
    v = v.reshape(B_, L, ATTN_HEADS, 2 * ATTN_HD).transpose(0, 2, 1, 3)
    f32 = jnp.float32
    lam = (jnp.exp(jnp.sum(lq1.astype(f32) * lk1.astype(f32)))
           - jnp.exp(jnp.sum(lq2.astype(f32) * lk2.astype(f32))) + lambda_init)
    n_blocks = L // Q_BLOCK
    q_blocks = q.reshape(B_, ATTN_HEADS, 2, n_blocks, Q_BLOCK, ATTN_HD).transpose(3, 0, 1, 2, 4, 5)
    k_pos = jnp.arange(L, dtype=jnp.int32)
    scale = ATTN_HD ** -0.5
    bias_table = rel_bias.astype(f32).T

    def block(args):
        qb, start = args
        q_pos = start + jnp.arange(Q_BLOCK, dtype=jnp.int32)
        rel = q_pos[:, None] - k_pos[None, :]
        s = jnp.einsum('bhmqd,bhmkd->bhmqk', qb, k).astype(f32) * scale
        s = s + bias_table[:, rel_bucket(rel)][None, :, None]
        s = jnp.where((rel >= 0)[None, None, None], s, -jnp.inf)
        p = jax.nn.softmax(s, axis=-1)
        p = p[:, :, 0] - lam * p[:, :, 1]
        return jnp.einsum('bhqk,bhkd->bhqd', p.astype(v.dtype), v)

    starts = jnp.arange(n_blocks, dtype=jnp.int32) * Q_BLOCK
    o = lax.map(block, (q_blocks, starts))
    o = o.transpose(1, 2, 0, 3, 4).reshape(B_, ATTN_HEADS, L, 2 * ATTN_HD)
    o = rms_norm(o, subln_g) * (1.0 - lambda_init)
    o = o.transpose(0, 2, 1, 3).reshape(B_, L, D_MODEL)
    return o @ w_o


def moe_ffn(h, router, w_gate_up, w_down):
    shape = h.shape
    hf = h.reshape(-1, D_MODEL)
    logits = (hf @ router).astype(jnp.float32)
    top_vals, top_idx = lax.top_k(logits, TOP_K)
    gates = jax.nn.softmax(top_vals, axis=-1)
    combine = jnp.sum(jax.nn.one_hot(top_idx, N_EXPERTS, dtype=jnp.float32) * gates[..., None], axis=1)
    y = jnp.zeros_like(hf)
    for e in range(N_EXPERTS):
        y = y + combine[:, e:e + 1].astype(hf.dtype) * swiglu(hf, w_gate_up[e], w_down[e])
    return y.reshape(shape)


def setup_inputs(seed: int = 0) -> dict:
    key = jax.random.key(seed)
    ks = jax.random.split(key, 24)
    nrm = jax.random.normal
    D, G, P, C, F, E = D_MODEL, SSM_GROUPS, SSM_STATE, SSM_GROUP, D_FF, N_EXPERTS
    NA, NB, ND, NM = N_SSM_LAYERS, N_ATTN_LAYERS, N_DENSE_LAYERS, N_MOE_LAYERS
    a_im0 = jnp.pi * jnp.arange(P, dtype=jnp.float32)
    return {
        "x": nrm(ks[0], (BATCH, SEQ, D), jnp.float32),
        "norm_mix_g": 1.0 + 0.01 * nrm(ks[1], (DEPTH, D)),
        "norm_ffn_g": 1.0 + 0.01 * nrm(ks[2], (DEPTH, D)),
        "final_norm_g": 1.0 + 0.01 * nrm(ks[3], (D,)),
        "ssm_a_re": -0.5 + 0.01 * nrm(ks[4], (NA, G, P)),
        "ssm_a_im": a_im0 + 0.01 * nrm(ks[5], (NA, G, P)),
        "ssm_log_dt": jax.random.uniform(ks[6], (NA, G), minval=math.log(1e-3), maxval=math.log(1e-1)),
        "ssm_b_re": nrm(ks[7], (NA, G, P, C)) * (2 * C) ** -0.5,
        "ssm_b_im": nrm(ks[8], (NA, G, P, C)) * (2 * C) ** -0.5,
        "ssm_c_re": nrm(ks[9], (NA, G, C, P)) * (2 * P) ** -0.5,
        "ssm_c_im": nrm(ks[10], (NA, G, C, P)) * (2 * P) ** -0.5,
        "ssm_d": nrm(ks[11], (NA, D)),
        "ssm_w_glu": nrm(ks[12], (NA, D, 2 * D)) * D ** -0.5,
        "attn_w_qkv": nrm(ks[13], (NB, D, 3 * D)) * D ** -0.5,
        "attn_lambda_q1": 0.1 * nrm(ks[14], (NB, ATTN_HD)),
        "attn_lambda_k1": 0.1 * nrm(ks[15], (NB, ATTN_HD)),
        "attn_lambda_q2": 0.1 * nrm(ks[16], (NB, ATTN_HD)),
        "attn_lambda_k2": 0.1 * nrm(ks[17], (NB, ATTN_HD)),
        "attn_subln_g": 1.0 + 0.01 * nrm(ks[18], (NB, 2 * ATTN_HD)),
        "attn_w_o": nrm(ks[19], (NB, D, D)) * D ** -0.5,
        "rel_bias": 0.5 * nrm(ks[20], (REL_BUCKETS, ATTN_HEADS)),
        "ffn_w_gate_up": nrm(ks[21], (ND, D, 2 * F)) * D ** -0.5,
        "ffn_w_down": nrm(ks[22], (ND, F, D)) * F ** -0.5,
        "moe_router": nrm(jax.random.fold_in(ks[23], 0), (NM, D, E)) * D ** -0.5,
        "moe_w_gate_up": nrm(jax.random.fold_in(ks[23], 1), (NM, E, D, 2 * F)) * D ** -0.5,
        "moe_w_down": nrm(jax.random.fold_in(ks[23], 2), (NM, E, F, D)) * F ** -0.5,
    }


def reference(x, norm_mix_g, norm_ffn_g, final_norm_g,
              ssm_a_re, ssm_a_im, ssm_log_dt, ssm_b_re, ssm_b_im, ssm_c_re, ssm_c_im, ssm_d, ssm_w_glu,
              attn_w_qkv, attn_lambda_q1, attn_lambda_k1, attn_lambda_q2, attn_lambda_k2,
              attn_subln_g, attn_w_o, rel_bias,
              ffn_w_gate_up, ffn_w_down, moe_router, moe_w_gate_up, moe_w_down):
    for i in range(DEPTH):
        j = i // 2
        h = rms_norm(x, norm_mix_g[i])
        if i % N_MIXERS == 0:
            y = s5_mixer(h, ssm_a_re[j], ssm_a_im[j], ssm_log_dt[j], ssm_b_re[j], ssm_b_im[j],
                         ssm_c_re[j], ssm_c_im[j], ssm_d[j])
            a, g = jnp.split(jax.nn.gelu(y) @ ssm_w_glu[j], 2, axis=-1)
            x = x + a * jax.nn.sigmoid(g)
        else:
            lambda_init = 0.8 - 0.6 * math.exp(-0.3 * i)
            x = x + diff_attention(h, attn_w_qkv[j], attn_lambda_q1[j], attn_lambda_k1[j],
                                   attn_lambda_q2[j], attn_lambda_k2[j], attn_subln_g[j],
                                   attn_w_o[j], rel_bias, lambda_init)
        h = rms_norm(x, norm_ffn_g[i])
        if i % 2 == 0:
            x = x + swiglu(h, ffn_w_gate_up[j], ffn_w_down[j])
        else:
            x = x + moe_ffn(h, moe_router[j], moe_w_gate_up[j], moe_w_down[j])
    return rms_norm(x, final_norm_g)
```

```python
import functools
import math

import jax
import jax.numpy as jnp
from jax import lax
from jax.experimental import pallas as pl
from jax.experimental.pallas import tpu as pltpu

F32 = jnp.float32
BF16 = jnp.bfloat16

RMS_EPS = 1e-6
N_HEADS = 8
HEAD_DIM = 64
N_EXPERTS = 8
TOP_K = 2
SSM_GROUP = 16
SSM_STATE = 64
REL_BUCKETS = 32
REL_MAX_EXACT = REL_BUCKETS // 2
REL_MAX_DIST = 128

LANES = 128
SUBLANES = 8
MXU_DIM = 256
VMEM_LIMIT_BYTES = 56 * 1024 * 1024

ROW_TILE = 512
ATTN_BLOCK = 512
SSM_TILE = 256
MOE_TILE = 512


def _params(*sem):
    return pltpu.CompilerParams(dimension_semantics=sem, vmem_limit_bytes=VMEM_LIMIT_BYTES)


def _const_spec(shape):
    nd = len(shape)
    return pl.BlockSpec(shape, lambda *_: (0,) * nd, pipeline_mode=pl.Buffered(1))


def _rms(x, g):
    ms = jnp.mean(x * x, axis=-1, keepdims=True)
    return x * lax.rsqrt(ms + RMS_EPS) * g


def _swiglu_acc(h, wgu_ref, wd_ref, d_ff, chunk):
    acc = None
    for c in range(d_ff // chunk):
        g = jnp.dot(h, wgu_ref[:, c * chunk:(c + 1) * chunk], preferred_element_type=F32)
        u = jnp.dot(h, wgu_ref[:, d_ff + c * chunk:d_ff + (c + 1) * chunk],
                    preferred_element_type=F32)
        a = (g * jax.nn.sigmoid(g) * u).astype(BF16)
        part = jnp.dot(a, wd_ref[c * chunk:(c + 1) * chunk, :], preferred_element_type=F32)
        acc = part if acc is None else acc + part
    return acc


def _qkv_kernel(x_ref, g_ref, w_ref, o_ref, *, tn, q_cols, q_scale):
    h = _rms(x_ref[...], g_ref[...]).astype(BF16)
    n = w_ref.shape[1]
    for j in range(n // tn):
        acc = jnp.dot(h, w_ref[:, j * tn:(j + 1) * tn], preferred_element_type=F32)
        if (j + 1) * tn <= q_cols:
            acc = acc * q_scale
        o_ref[:, j * tn:(j + 1) * tn] = acc.astype(o_ref.dtype)


def _qkv_proj(x, g, w):
    t, d = x.shape
    n = w.shape[1]
    tm = min(ROW_TILE, t)
    kern = functools.partial(_qkv_kernel, tn=512, q_cols=d, q_scale=HEAD_DIM ** -0.5)
    return pl.pallas_call(
        kern,
        grid=(t // tm,),
        in_specs=[pl.BlockSpec((tm, d), lambda i: (i, 0)),
                  _const_spec((1, d)),
                  _const_spec((d, n))],
        out_specs=pl.BlockSpec((tm, n), lambda i: (i, 0)),
        out_shape=jax.ShapeDtypeStruct((t, n), BF16),
        compiler_params=_params("parallel"),
        name="qkv_proj",
    )(x, g, w)


def _attn_kernel(q_ref, k_ref, v_ref, bias_ref, lqk_ref, sg_ref, o_ref,
                 m1_ref, l1_ref, a1_ref, m2_ref, l2_ref, a2_ref, *, blk, lambda_init):
    qi = pl.program_id(2)
    q = q_ref[...]
    lane = lax.broadcasted_iota(jnp.int32, q.shape, 1)
    zero = jnp.zeros_like(q)
    q1 = jnp.where(lane < HEAD_DIM, q, zero)
    q2 = jnp.where(lane >= HEAD_DIM, q, zero)

    for m_ref, l_ref, a_ref in ((m1_ref, l1_ref, a1_ref), (m2_ref, l2_ref, a2_ref)):
        m_ref[...] = jnp.full(m_ref.shape, -1e30, F32)
        l_ref[...] = jnp.zeros(l_ref.shape, F32)
        a_ref[...] = jnp.zeros(a_ref.shape, F32)

    def step(j, bias):
        start = pl.multiple_of(j * blk, blk)
        k = k_ref[pl.ds(start, blk), :]
        v = v_ref[pl.ds(start, blk), :]
        for qm, m_ref, l_ref, a_ref in ((q1, m1_ref, l1_ref, a1_ref),
                                        (q2, m2_ref, l2_ref, a2_ref)):
            s = lax.dot_general(qm, k, (((1,), (1,)), ((), ())), preferred_element_type=F32)
            if bias is not None:
                s = s + bias
            m_old = m_ref[...]
            m_new = jnp.maximum(m_old, jnp.max(s, axis=-1, keepdims=True))
            alpha = jnp.exp(m_old - m_new)
            p = jnp.exp(s - m_new)
            l_ref[...] = alpha * l_ref[...] + jnp.sum(p, axis=-1, keepdims=True)
            a_ref[...] = alpha * a_ref[...] + jnp.dot(p.astype(BF16), v,
                                                     preferred_element_type=F32)
            m_ref[...] = m_new

    def far_body(j, carry):
        step(j, None)
        return carry

    lax.fori_loop(0, jnp.maximum(qi - 1, 0), far_body, 0)

    @pl.when(qi >= 1)
    def _():
        step(qi - 1, bias_ref[1])

    step(qi, bias_ref[0])

    lq = lqk_ref[...]
    lam = (jnp.exp(jnp.sum(lq[0:1] * lq[1:2], keepdims=True))
           - jnp.exp(jnp.sum(lq[2:3] * lq[3:4], keepdims=True)) + lambda_init)
    o = a1_ref[...] / l1_ref[...] - lam * (a2_ref[...] / l2_ref[...])
    o = _rms(o, sg_ref[...]) * (1.0 - lambda_init)
    o_ref[...] = o.astype(o_ref.dtype)


def _attention(qkv, bias_tiles, lqk, subln_g, batch, lambda_init):
    t, three_d = qkv.shape
    d = three_d // 3
    seq = t // batch
    blk = min(ATTN_BLOCK, seq)
    nq = seq // blk
    hd2 = 2 * HEAD_DIM
    kern = functools.partial(_attn_kernel, blk=blk, lambda_init=lambda_init)
    return pl.pallas_call(
        kern,
        grid=(batch, N_HEADS, nq),
        in_specs=[
            pl.BlockSpec((blk, hd2), lambda b, h, i: (b * nq + i, h)),
            pl.BlockSpec((seq, hd2), lambda b, h, i: (b, N_HEADS + h)),
            pl.BlockSpec((seq, hd2), lambda b, h, i: (b, 2 * N_HEADS + h)),
            pl.BlockSpec((None, 2, blk, blk), lambda b, h, i: (h, 0, 0, 0)),
            pl.BlockSpec((4, HEAD_DIM), lambda b, h, i: (0, 0)),
            pl.BlockSpec((1, hd2), lambda b, h, i: (0, 0)),
        ],
        out_specs=pl.BlockSpec((blk, hd2), lambda b, h, i: (b * nq + i, h)),
        out_shape=jax.ShapeDtypeStruct((t, d), BF16),
        scratch_shapes=[pltpu.VMEM((blk, 1), F32), pltpu.VMEM((blk, 1), F32),
                        pltpu.VMEM((blk, hd2), F32),
                        pltpu.VMEM((blk, 1), F32), pltpu.VMEM((blk, 1), F32),
                        pltpu.VMEM((blk, hd2), F32)],
        compiler_params=_params("parallel", "parallel", "arbitrary"),
        name="diff_attention",
    )(qkv, qkv, qkv, bias_tiles, lqk, subln_g)


def _rel_bucket(n):
    n = jnp.maximum(n, 0)
    nf = jnp.maximum(n, 1).astype(F32)
    large = REL_MAX_EXACT + (jnp.log(nf / REL_MAX_EXACT) / math.log(REL_MAX_DIST / REL_MAX_EXACT)
                             * (REL_BUCKETS - REL_MAX_EXACT)).astype(jnp.int32)
    large = jnp.minimum(large, REL_BUCKETS - 1)
    return jnp.where(n < REL_MAX_EXACT, n, large)


def _bias_tiles(rel_bias, blk):
    table = rel_bias.astype(F32).T
    table = table - table[:, REL_BUCKETS - 1:]
    i = jnp.arange(blk, dtype=jnp.int32)
    rel0 = i[:, None] - i[None, :]
    rel1 = rel0 + blk
    diag = jnp.where((rel0 >= 0)[None], table[:, _rel_bucket(rel0)], -jnp.inf)
    sub = table[:, _rel_bucket(rel1)]
    return jnp.stack([diag, sub], axis=1)


def _wo_router_kernel(a_ref, x_ref, wo_ref, g_ref, r_ref, x1_ref, route_ref):
    x1 = x_ref[...] + jnp.dot(a_ref[...], wo_ref[...], preferred_element_type=F32)
    x1_ref[...] = x1
    h = _rms(x1, g_ref[...])
    logits = jnp.dot(h, r_ref[...], preferred_element_type=F32,
                     precision=lax.Precision.HIGHEST)
    lane = lax.broadcasted_iota(jnp.int32, logits.shape, 1).astype(F32)
    neg = jnp.float32(-jnp.inf)
    logits = jnp.where(lane < N_EXPERTS, logits, neg)
    m1 = jnp.max(logits, axis=-1, keepdims=True)
    i1 = jnp.min(jnp.where(logits == m1, lane, float(LANES)), axis=-1, keepdims=True)
    rest = jnp.where(lane == i1, neg, logits)
    m2 = jnp.max(rest, axis=-1, keepdims=True)
    i2 = jnp.min(jnp.where(rest == m2, lane, float(LANES)), axis=-1, keepdims=True)
    e = jnp.exp(m2 - m1)
    g1 = 1.0 / (1.0 + e)
    g2 = e / (1.0 + e)
    route = jnp.where(lane == 0, i1,
                      jnp.where(lane == 1, i2,
                                jnp.where(lane == 2, g1, jnp.where(lane == 3, g2, 0.0))))
    route_ref[...] = route


def _wo_router(attn, x, wo, g, router_pad):
    t, d = x.shape
    tm = min(ROW_TILE, t)
    return pl.pallas_call(
        _wo_router_kernel,
        grid=(t // tm,),
        in_specs=[pl.BlockSpec((tm, d), lambda i: (i, 0)),
                  pl.BlockSpec((tm, d), lambda i: (i, 0)),
                  _const_spec((d, d)),
                  _const_spec((1, d)),
                  _const_spec((d, LANES))],
        out_specs=[pl.BlockSpec((tm, d), lambda i: (i, 0)),
                   pl.BlockSpec((tm, LANES), lambda i: (i, 0))],
        out_shape=[jax.ShapeDtypeStruct((t, d), F32),
                   jax.ShapeDtypeStruct((t, LANES), F32)],
        compiler_params=_params("parallel"),
        name="wo_router",
    )(attn, x, wo, g, router_pad)


def _ffn_kernel(x_ref, g_ref, wgu_ref, wd_ref, o_ref, *, d_ff, chunk):
    x = x_ref[...]
    h = _rms(x, g_ref[...]).astype(BF16)
    o_ref[...] = x + _swiglu_acc(h, wgu_ref, wd_ref, d_ff, chunk)


def _ffn_chunk(d_ff):
    return d_ff // 2 if (d_ff // 2) % LANES == 0 else d_ff


def _dense_ffn(x, g, wgu, wd):
    t, d = x.shape
    d_ff = wd.shape[0]
    tm = min(ROW_TILE, t)
    kern = functools.partial(_ffn_kernel, d_ff=d_ff, chunk=_ffn_chunk(d_ff))
    return pl.pallas_call(
        kern,
        grid=(t // tm,),
        in_specs=[pl.BlockSpec((tm, d), lambda i: (i, 0)),
                  _const_spec((1, d)),
                  _const_spec((d, 2 * d_ff)),
                  _const_spec((d_ff, d))],
        out_specs=pl.BlockSpec((tm, d), lambda i: (i, 0)),
        out_shape=jax.ShapeDtypeStruct((t, d), F32),
        compiler_params=_params("parallel"),
        name="dense_ffn",
    )(x, g, wgu, wd)


def _moe_kernel(te_ref, tv_ref, src_ref, dst_ref, gate_ref, x_hbm, g_ref, wgu_ref, wd_ref,
                out_hbm, xbuf, ybuf, sem, *, tm, d_ff, chunk):
    i = pl.program_id(0)

    @pl.when(tv_ref[i] > 0)
    def _():
        def gather(r, carry):
            t = src_ref[0, 0, r]
            pltpu.make_async_copy(x_hbm.at[pl.ds(t, 1), :], xbuf.at[pl.ds(r, 1), :],
                                  sem.at[0]).start()
            return carry
        lax.fori_loop(0, tm, gather, 0)
        pltpu.make_async_copy(x_hbm.at[pl.ds(0, tm), :], xbuf, sem.at[0]).wait()

        h = _rms(xbuf[...], g_ref[...]).astype(BF16)
        ybuf[...] = _swiglu_acc(h, wgu_ref, wd_ref, d_ff, chunk) * gate_ref[...]

    @pl.when(tv_ref[i] == 0)
    def _():
        ybuf[...] = jnp.zeros(ybuf.shape, F32)

    def scatter(r, carry):
        t = dst_ref[0, 0, r]
        pltpu.make_async_copy(ybuf.at[pl.ds(r, 1), :], out_hbm.at[pl.ds(t, 1), :],
                              sem.at[1]).start()
        return carry
    lax.fori_loop(0, tm, scatter, 0)
    pltpu.make_async_copy(ybuf, out_hbm.at[pl.ds(0, tm), :], sem.at[1]).wait()


def _moe_ffn(x1, route, g, wgu, wd):
    t, d = x1.shape
    d_ff = wd.shape[1]
    tm = min(MOE_TILE, t)
    n_pairs = TOP_K * t
    n_tiles = n_pairs // tm + N_EXPERTS
    n_rows = n_tiles * tm

    idx = route[:, :TOP_K].astype(jnp.int32).reshape(-1)
    gates = route[:, TOP_K:2 * TOP_K].reshape(-1)
    onehot = (idx[:, None] == jnp.arange(N_EXPERTS, dtype=jnp.int32)[None, :]).astype(jnp.int32)
    rank = jnp.take_along_axis(jnp.cumsum(onehot, axis=0), idx[:, None], axis=1)[:, 0] - 1
    counts = jnp.sum(onehot, axis=0)
    tiles_per = (counts + tm - 1) // tm
    tile_end = jnp.cumsum(tiles_per)
    seg_start = (tile_end - tiles_per) * tm
    pos = seg_start[idx] + rank
    pair_tok = jnp.arange(n_pairs, dtype=jnp.int32) // TOP_K
    pair_dst = (jnp.arange(n_pairs, dtype=jnp.int32) % TOP_K) * t + pair_tok
    valid = jnp.zeros((n_rows,), jnp.int32).at[pos].set(1)
    spare = n_pairs + jnp.cumsum(1 - valid) - 1
    row_src = jnp.zeros((n_rows,), jnp.int32).at[pos].set(pair_tok)
    row_dst = jnp.where(valid == 1, jnp.zeros((n_rows,), jnp.int32).at[pos].set(pair_dst), spare)
    row_gate = jnp.zeros((n_rows,), F32).at[pos].set(gates)
    tile_ids = jnp.arange(n_tiles, dtype=jnp.int32)
    tile_expert = jnp.minimum(jnp.searchsorted(tile_end, tile_ids, side="right"),
                              N_EXPERTS - 1).astype(jnp.int32)
    n_used = tile_end[-1]
    tile_valid = (tile_ids < n_used).astype(jnp.int32)
    last_expert = tile_expert[jnp.maximum(n_used - 1, 0)]
    tile_expert = jnp.where(tile_valid == 1, tile_expert, last_expert)

    kern = functools.partial(_moe_kernel, tm=tm, d_ff=d_ff, chunk=_ffn_chunk(d_ff))
    grid_spec = pltpu.PrefetchScalarGridSpec(
        num_scalar_prefetch=2,
        grid=(n_tiles,),
        in_specs=[
            pl.BlockSpec((1, 1, tm), lambda i, te, tv: (i, 0, 0), memory_space=pltpu.SMEM),
            pl.BlockSpec((1, 1, tm), lambda i, te, tv: (i, 0, 0), memory_space=pltpu.SMEM),
            pl.BlockSpec((tm, 1), lambda i, te, tv: (i, 0)),
            pl.BlockSpec(memory_space=pl.ANY),
            pl.BlockSpec((1, d), lambda i, te, tv: (0, 0)),
            pl.BlockSpec((None, d, 2 * d_ff), lambda i, te, tv: (te[i], 0, 0)),
            pl.BlockSpec((None, d_ff, d), lambda i, te, tv: (te[i], 0, 0)),
        ],
        out_specs=pl.BlockSpec(memory_space=pl.ANY),
        scratch_shapes=[pltpu.VMEM((tm, d), F32), pltpu.VMEM((tm, d), F32),
                        pltpu.SemaphoreType.DMA((2,))],
    )
    return pl.pallas_call(
        kern,
        grid_spec=grid_spec,
        out_shape=jax.ShapeDtypeStruct((n_rows, d), F32),
        compiler_params=_params("arbitrary"),
        name="moe_ffn",
    )(tile_expert, tile_valid, row_src.reshape(n_tiles, 1, tm), row_dst.reshape(n_tiles, 1, tm),
      row_gate.reshape(n_rows, 1), x1, g, wgu, wd)


def _combine_kernel(x_ref, y0_ref, y1_ref, g_ref, o_ref, *, final_norm):
    x = x_ref[...] + y0_ref[...] + y1_ref[...]
    o_ref[...] = _rms(x, g_ref[...]) if final_norm else x


def _moe_combine(x1, out2, g, final_norm):
    t, d = x1.shape
    tm = min(ROW_TILE, t)
    nb = t // tm
    kern = functools.partial(_combine_kernel, final_norm=final_norm)
    return pl.pallas_call(
        kern,
        grid=(nb,),
        in_specs=[pl.BlockSpec((tm, d), lambda i: (i, 0)),
                  pl.BlockSpec((tm, d), lambda i: (i, 0)),
                  pl.BlockSpec((tm, d), lambda i: (nb + i, 0)),
                  _const_spec((1, d))],
        out_specs=pl.BlockSpec((tm, d), lambda i: (i, 0)),
        out_shape=jax.ShapeDtypeStruct((t, d), F32),
        compiler_params=_params("parallel"),
        name="moe_combine",
    )(x1, out2, out2, g)


def _ssm_kernel(x_ref, g_ref, bd_ref, cd_ref, a_ref, pw_ref, d_ref, wglu_ref, o_ref,
                xp_ref, xc_ref, xs_ref, init_ref, carry_ref, *, tt, n_kb):
    seg = tt // SUBLANES
    half = xs_ref.shape[1] // (2 * n_kb)
    d = x_ref.shape[1]
    kb_in = d // n_kb

    @pl.when(pl.program_id(1) == 0)
    def _():
        carry_ref[...] = jnp.zeros(carry_ref.shape, F32)

    n_lc = d // LANES
    for c in range(n_lc):
        xc_ref[c] = x_ref[:, c * LANES:(c + 1) * LANES]
    for i in range(seg):
        for c in range(n_lc):
            xp_ref[pl.ds(SUBLANES * i, SUBLANES), c * LANES:(c + 1) * LANES] = (
                xc_ref[c, pl.ds(i, SUBLANES, stride=seg), :])

    xp = xp_ref[...]
    u = _rms(xp, g_ref[...])
    ub = u.astype(BF16)

    y_parts = []
    for kb in range(n_kb):
        c0 = 2 * half * kb
        xs_ref[:, c0:c0 + 2 * half] = jnp.dot(ub[:, kb * kb_in:(kb + 1) * kb_in], bd_ref[kb],
                                              preferred_element_type=F32)
        a_re = jnp.broadcast_to(a_ref[0:1, kb * half:(kb + 1) * half], (SUBLANES, half))
        a_im = jnp.broadcast_to(a_ref[1:2, kb * half:(kb + 1) * half], (SUBLANES, half))

        def scan_body(i, s, c0=c0, a_re=a_re, a_im=a_im):
            s_re, s_im = s
            r0 = pl.multiple_of(i * SUBLANES, SUBLANES)
            x_re = xs_ref[pl.ds(r0, SUBLANES), c0:c0 + half]
            x_im = xs_ref[pl.ds(r0, SUBLANES), c0 + half:c0 + 2 * half]
            n_re = a_re * s_re - a_im * s_im + x_re
            n_im = a_re * s_im + a_im * s_re + x_im
            xs_ref[pl.ds(r0, SUBLANES), c0:c0 + half] = n_re
            xs_ref[pl.ds(r0, SUBLANES), c0 + half:c0 + 2 * half] = n_im
            return n_re, n_im

        zero = jnp.zeros((SUBLANES, half), F32)
        e_re, e_im = lax.fori_loop(0, seg, scan_body, (zero, zero))

        al_re = a_ref[2:3, kb * half:(kb + 1) * half]
        al_im = a_ref[3:4, kb * half:(kb + 1) * half]
        in_re = carry_ref[0:1, kb * half:(kb + 1) * half]
        in_im = carry_ref[1:2, kb * half:(kb + 1) * half]
        for j in range(SUBLANES):
            init_ref[j:j + 1, 0:half] = in_re
            init_ref[j:j + 1, half:2 * half] = in_im
            n_re = e_re[j:j + 1] + al_re * in_re - al_im * in_im
            n_im = e_im[j:j + 1] + al_re * in_im + al_im * in_re
            in_re, in_im = n_re, n_im
        carry_ref[0:1, kb * half:(kb + 1) * half] = in_re
        carry_ref[1:2, kb * half:(kb + 1) * half] = in_im
        i_re = init_ref[:, 0:half]
        i_im = init_ref[:, half:2 * half]

        def fix_body(i, carry, c0=c0, i_re=i_re, i_im=i_im, kb=kb):
            r0 = pl.multiple_of(i * SUBLANES, SUBLANES)
            p_re = pw_ref[0, pl.ds(i, 1), kb * half:(kb + 1) * half]
            p_im = pw_ref[1, pl.ds(i, 1), kb * half:(kb + 1) * half]
            s_re = xs_ref[pl.ds(r0, SUBLANES), c0:c0 + half]
            s_im = xs_ref[pl.ds(r0, SUBLANES), c0 + half:c0 + 2 * half]
            xs_ref[pl.ds(r0, SUBLANES), c0:c0 + half] = s_re + p_re * i_re - p_im * i_im
            xs_ref[pl.ds(r0, SUBLANES), c0 + half:c0 + 2 * half] = s_im + p_re * i_im + p_im * i_re
            return carry

        lax.fori_loop(0, seg, fix_body, 0)
        y_parts.append(jnp.dot(xs_ref[:, c0:c0 + 2 * half].astype(BF16), cd_ref[kb],
                               preferred_element_type=F32))

    y = jnp.concatenate(y_parts, axis=1) + d_ref[...] * u
    z = jnp.dot(jax.nn.gelu(y).astype(BF16), wglu_ref[...], preferred_element_type=F32)
    res = xp + z[:, :d] * jax.nn.sigmoid(z[:, d:])
    xp_ref[...] = res
    for i in range(seg):
        for c in range(n_lc):
            xc_ref[c, pl.ds(i, SUBLANES, stride=seg), :] = (
                xp_ref[pl.ds(SUBLANES * i, SUBLANES), c * LANES:(c + 1) * LANES])
    for c in range(n_lc):
        o_ref[:, c * LANES:(c + 1) * LANES] = xc_ref[c]


def _ssm_tables(a_re, a_im, log_dt, b_re, b_im, c_re, c_im, seg, n_kb):
    g_, p_ = a_re.shape
    c_ = b_re.shape[-1]
    gk = g_ // n_kb
    dt = jnp.exp(log_dt.astype(F32))[:, None]
    ar = a_re.astype(F32)
    ai = a_im.astype(F32)
    mag = jnp.exp(ar * dt)
    ab_re = mag * jnp.cos(ai * dt)
    ab_im = mag * jnp.sin(ai * dt)
    den = ar * ar + ai * ai
    nr = ab_re - 1.0
    ni = ab_im
    coef_re = (nr * ar + ni * ai) / den
    coef_im = (ni * ar - nr * ai) / den
    br = b_re.astype(F32)
    bi = b_im.astype(F32)
    bb_re = coef_re[..., None] * br - coef_im[..., None] * bi
    bb_im = coef_re[..., None] * bi + coef_im[..., None] * br
    eye = jnp.eye(gk, dtype=F32)

    def blockdiag_in(w):
        w = w.reshape(n_kb, gk, p_, c_)
        return jnp.einsum("kgpc,gh->kgchp", w, eye).reshape(n_kb, gk * c_, gk * p_)

    def blockdiag_out(w):
        w = w.reshape(n_kb, gk, c_, p_)
        return jnp.einsum("kgcp,gh->kgphc", w, eye).reshape(n_kb, gk * p_, gk * c_)

    bd = jnp.concatenate([blockdiag_in(bb_re), blockdiag_in(bb_im)], axis=2).astype(BF16)
    cd = jnp.concatenate([blockdiag_out(c_re.astype(F32)),
                          -blockdiag_out(c_im.astype(F32))], axis=1).astype(BF16)

    def cpow(n):
        m = jnp.exp(ar * dt * n)
        return m * jnp.cos(ai * dt * n), m * jnp.sin(ai * dt * n)

    al_re, al_im = cpow(float(seg))
    a_tab = jnp.stack([ab_re.reshape(-1), ab_im.reshape(-1),
                       al_re.reshape(-1), al_im.reshape(-1)], axis=0)
    n = jnp.arange(1, seg + 1, dtype=F32)[:, None, None]
    pm = jnp.exp(ar[None] * dt[None] * n)
    pw = jnp.stack([(pm * jnp.cos(ai[None] * dt[None] * n)).reshape(seg, -1),
                    (pm * jnp.sin(ai[None] * dt[None] * n)).reshape(seg, -1)], axis=0)
    return bd, cd, a_tab, pw


def _ssm_mixer(x, g, tables, d_skip, wglu, batch):
    t, d = x.shape
    seq = t // batch
    tt = min(SSM_TILE, seq)
    nt = seq // tt
    bd, cd, a_tab, pw = tables
    n_kb = bd.shape[0]
    n_state = a_tab.shape[1]
    kern = functools.partial(_ssm_kernel, tt=tt, n_kb=n_kb)
    return pl.pallas_call(
        kern,
        grid=(batch, nt),
        in_specs=[pl.BlockSpec((tt, d), lambda b, n: (b * nt + n, 0)),
                  _const_spec((1, d)),
                  _const_spec(bd.shape),
                  _const_spec(cd.shape),
                  _const_spec(a_tab.shape),
                  _const_spec(pw.shape),
                  _const_spec((1, d)),
                  _const_spec(wglu.shape)],
        out_specs=pl.BlockSpec((tt, d), lambda b, n: (b * nt + n, 0)),
        out_shape=jax.ShapeDtypeStruct((t, d), F32),
        scratch_shapes=[pltpu.VMEM((tt, d), F32),
                        pltpu.VMEM((d // LANES, tt, LANES), F32),
                        pltpu.VMEM((tt, 2 * n_state), F32),
                        pltpu.VMEM((SUBLANES, 2 * n_state // n_kb), F32),
                        pltpu.VMEM((2, n_state), F32)],
        compiler_params=_params("parallel", "arbitrary"),
        name="ssm_mixer",
    )(x, g, bd, cd, a_tab, pw, d_skip, wglu)


def kernel(x, norm_mix_g, norm_ffn_g, final_norm_g, ssm_a_re, ssm_a_im, ssm_log_dt, ssm_b_re,
           ssm_b_im, ssm_c_re, ssm_c_im, ssm_d, ssm_w_glu, attn_w_qkv, attn_lambda_q1,
           attn_lambda_k1, attn_lambda_q2, attn_lambda_k2, attn_subln_g, attn_w_o, rel_bias,
           ffn_w_gate_up, ffn_w_down, moe_router, moe_w_gate_up, moe_w_down):
    batch, seq, d = x.shape
    depth = norm_mix_g.shape[0]
    xf = x.reshape(batch * seq, d).astype(F32)
    row = lambda v: v.reshape(1, -1).astype(F32)

    ssm_tt = min(SSM_TILE, seq)
    n_kb = d // MXU_DIM
    blk = min(ATTN_BLOCK, seq)
    bias_tiles = _bias_tiles(rel_bias, blk)

    for i in range(depth):
        j = i // 2
        if i % 2 == 0:
            tables = _ssm_tables(ssm_a_re[j], ssm_a_im[j], ssm_log_dt[j], ssm_b_re[j],
                                 ssm_b_im[j], ssm_c_re[j], ssm_c_im[j],
                                 ssm_tt // SUBLANES, n_kb)
            xf = _ssm_mixer(xf, row(norm_mix_g[i]), tables, row(ssm_d[j]),
                            ssm_w_glu[j].astype(BF16), batch)
            xf = _dense_ffn(xf, row(norm_ffn_g[i]), ffn_w_gate_up[j].astype(BF16),
                            ffn_w_down[j].astype(BF16))
        else:
            lambda_init = 0.8 - 0.6 * math.exp(-0.3 * i)
            qkv = _qkv_proj(xf, row(norm_mix_g[i]), attn_w_qkv[j].astype(BF16))
            lqk = jnp.stack([attn_lambda_q1[j], attn_lambda_k1[j],
                             attn_lambda_q2[j], attn_lambda_k2[j]], axis=0).astype(F32)
            attn = _attention(qkv, bias_tiles, lqk, row(attn_subln_g[j]), batch, lambda_init)
            router_pad = jnp.pad(moe_router[j].astype(F32), ((0, 0), (0, LANES - N_EXPERTS)))
            x1, route = _wo_router(attn, xf, attn_w_o[j].astype(BF16), row(norm_ffn_g[i]),
                                   router_pad)
            out2 = _moe_ffn(x1, route, row(norm_ffn_g[i]), moe_w_gate_up[j].astype(BF16),
                            moe_w_down[j].astype(BF16))
            last = i == depth - 1
            xf = _moe_combine(x1, out2, row(final_norm_g), final_norm=last)
    if depth % 2 == 1:
        raise NotImplementedError("trunk depth must be even (final norm is fused into the MoE combine)")
    return xf.reshape(batch, seq, d).astype(x.dtype)
```

```python
import functools
import math

import jax
import jax.numpy as jnp
from jax import lax
from jax.experimental import pallas as pl
from jax.experimental.pallas import tpu as pltpu

F32 = jnp.float32
BF16 = jnp.bfloat16

RMS_EPS = 1e-6
N_HEADS = 8
HEAD_DIM = 64
N_EXPERTS = 8
TOP_K = 2
SSM_GROUP = 16
SSM_STATE = 64
REL_BUCKETS = 32
REL_MAX_EXACT = REL_BUCKETS // 2
REL_MAX_DIST = 128
LOG2E = math.log2(math.e)

LANES = 128
SUBLANES = 8
MXU_DIM = 256
VMEM_LIMIT_BYTES = 56 * 1024 * 1024

ROW_TILE = 512
ATTN_BLOCK = 512
SSM_TILE = 256
MOE_TILE = 512


def _params(*sem):
    return pltpu.CompilerParams(dimension_semantics=sem, vmem_limit_bytes=VMEM_LIMIT_BYTES)


def _const_spec(shape):
    nd = len(shape)
    return pl.BlockSpec(shape, lambda *_: (0,) * nd, pipeline_mode=pl.Buffered(1))


def _rms(x, g):
    ms = jnp.mean(x * x, axis=-1, keepdims=True)
    return x * lax.rsqrt(ms + RMS_EPS) * g


def _swiglu_acc(h, wgu_ref, wd_ref, d_ff, chunk):
    acc = None
    for c in range(d_ff // chunk):
        g = jnp.dot(h, wgu_ref[:, c * chunk:(c + 1) * chunk], preferred_element_type=F32)
        u = jnp.dot(h, wgu_ref[:, d_ff + c * chunk:d_ff + (c + 1) * chunk],
                    preferred_element_type=F32)
        a = (g * jax.nn.sigmoid(g) * u).astype(BF16)
        part = jnp.dot(a, wd_ref[c * chunk:(c + 1) * chunk, :], preferred_element_type=F32)
        acc = part if acc is None else acc + part
    return acc


def _qkv_kernel(x_ref, g_ref, w_ref, o_ref, *, tn, q_cols, q_scale):
    h = _rms(x_ref[...], g_ref[...]).astype(BF16)
    n = w_ref.shape[1]
    for j in range(n // tn):
        acc = jnp.dot(h, w_ref[:, j * tn:(j + 1) * tn], preferred_element_type=F32)
        if (j + 1) * tn <= q_cols:
            acc = acc * q_scale
        o_ref[:, j * tn:(j + 1) * tn] = acc.astype(o_ref.dtype)


def _qkv_proj(x, g, w):
    t, d = x.shape
    n = w.shape[1]
    tm = min(ROW_TILE, t)
    kern = functools.partial(_qkv_kernel, tn=512, q_cols=d, q_scale=HEAD_DIM ** -0.5 * LOG2E)
    return pl.pallas_call(
        kern,
        grid=(t // tm,),
        in_specs=[pl.BlockSpec((tm, d), lambda i: (i, 0)),
                  _const_spec((1, d)),
                  _const_spec((d, n))],
        out_specs=pl.BlockSpec((tm, n), lambda i: (i, 0)),
        out_shape=jax.ShapeDtypeStruct((t, n), BF16),
        compiler_params=_params("parallel"),
        name="qkv_proj",
    )(x, g, w)


def _attn_kernel(q_ref, k_ref, v_ref, bias_ref, lqk_ref, sg_ref, o_ref,
                 qs_ref, m_ref, acc_ref, *, blk, lambda_init):
    qi = pl.program_id(2)
    q = q_ref[...]
    lane = lax.broadcasted_iota(jnp.int32, q.shape, 1)
    zero = jnp.zeros_like(q)
    qs_ref[0:blk, :] = jnp.where(lane < HEAD_DIM, q, zero)
    qs_ref[blk:2 * blk, :] = jnp.where(lane >= HEAD_DIM, q, zero)
    m_ref[...] = jnp.full(m_ref.shape, -1e30, F32)
    acc_ref[...] = jnp.zeros(acc_ref.shape, F32)
    ones = jnp.ones((blk, LANES), BF16)
    n_chunks = blk // LANES

    def step(j, bias):
        start = pl.multiple_of(j * blk, blk)
        k = k_ref[pl.ds(start, blk), :]
        v_ext = jnp.concatenate([v_ref[pl.ds(start, blk), :], ones], axis=1)
        s = lax.dot_general(qs_ref[...], k, (((1,), (1,)), ((), ())),
                            preferred_element_type=F32)
        if bias is not None:
            s = s + jnp.concatenate([bias, bias], axis=0)
        chunks = [s[:, c * LANES:(c + 1) * LANES] for c in range(n_chunks)]
        cm = chunks[0]
        for ch in chunks[1:]:
            cm = jnp.maximum(cm, ch)
        m_old = m_ref[...]
        m_new = jnp.maximum(m_old, jnp.max(cm, axis=-1, keepdims=True))
        alpha = jnp.exp2(m_old - m_new)
        p = jnp.concatenate([jnp.exp2(ch - m_new) for ch in chunks], axis=1).astype(BF16)
        pv = jnp.dot(p, v_ext, preferred_element_type=F32)
        acc_ref[...] = jnp.concatenate([alpha, alpha], axis=1) * acc_ref[...] + pv
        m_ref[...] = m_new

    def far_body(j, carry):
        step(j, None)
        return carry

    lax.fori_loop(0, jnp.maximum(qi - 1, 0), far_body, 0)

    @pl.when(qi >= 1)
    def _():
        step(qi - 1, bias_ref[1])

    step(qi, bias_ref[0])

    lq = lqk_ref[...]
    lam = (jnp.exp(jnp.sum(lq[0:1] * lq[1:2], keepdims=True))
           - jnp.exp(jnp.sum(lq[2:3] * lq[3:4], keepdims=True)) + lambda_init)
    acc = acc_ref[...]
    o1 = acc[0:blk, 0:LANES] / acc[0:blk, LANES:2 * LANES]
    o2 = acc[blk:2 * blk, 0:LANES] / acc[blk:2 * blk, LANES:2 * LANES]
    o = _rms(o1 - lam * o2, sg_ref[...]) * (1.0 - lambda_init)
    o_ref[...] = o.astype(o_ref.dtype)


def _attention(qkv, bias_tiles, lqk, subln_g, batch, lambda_init):
    t, three_d = qkv.shape
    d = three_d // 3
    seq = t // batch
    blk = min(ATTN_BLOCK, seq)
    nq = seq // blk
    hd2 = 2 * HEAD_DIM
    assert blk >= REL_MAX_DIST and seq % blk == 0 and blk % LANES == 0 and hd2 == LANES
    kern = functools.partial(_attn_kernel, blk=blk, lambda_init=lambda_init)
    return pl.pallas_call(
        kern,
        grid=(batch, N_HEADS, nq),
        in_specs=[
            pl.BlockSpec((blk, hd2), lambda b, h, i: (b * nq + i, h)),
            pl.BlockSpec((seq, hd2), lambda b, h, i: (b, N_HEADS + h)),
            pl.BlockSpec((seq, hd2), lambda b, h, i: (b, 2 * N_HEADS + h)),
            pl.BlockSpec((None, 2, blk, blk), lambda b, h, i: (h, 0, 0, 0)),
            pl.BlockSpec((4, HEAD_DIM), lambda b, h, i: (0, 0)),
            pl.BlockSpec((1, hd2), lambda b, h, i: (0, 0)),
        ],
        out_specs=pl.BlockSpec((blk, hd2), lambda b, h, i: (b * nq + i, h)),
        out_shape=jax.ShapeDtypeStruct((t, d), BF16),
        scratch_shapes=[pltpu.VMEM((2 * blk, hd2), BF16),
                        pltpu.VMEM((2 * blk, LANES), F32),
                        pltpu.VMEM((2 * blk, 2 * LANES), F32)],
        compiler_params=_params("parallel", "parallel", "arbitrary"),
        name="diff_attention",
    )(qkv, qkv, qkv, bias_tiles, lqk, subln_g)


def _rel_bucket(n):
    n = jnp.maximum(n, 0)
    nf = jnp.maximum(n, 1).astype(F32)
    large = REL_MAX_EXACT + (jnp.log(nf / REL_MAX_EXACT) / math.log(REL_MAX_DIST / REL_MAX_EXACT)
                             * (REL_BUCKETS - REL_MAX_EXACT)).astype(jnp.int32)
    large = jnp.minimum(large, REL_BUCKETS - 1)
    return jnp.where(n < REL_MAX_EXACT, n, large)


def _bias_tiles(rel_bias, blk):
    table = rel_bias.astype(F32).T
    table = (table - table[:, REL_BUCKETS - 1:]) * LOG2E
    i = jnp.arange(blk, dtype=jnp.int32)
    rel0 = i[:, None] - i[None, :]
    rel1 = rel0 + blk

    def lookup(rel):
        onehot = jax.nn.one_hot(_rel_bucket(rel), REL_BUCKETS, dtype=F32)
        return jnp.einsum("hk,ijk->hij", table, onehot, precision=lax.Precision.HIGHEST)

    diag = jnp.where((rel0 >= 0)[None], lookup(rel0), -jnp.inf)
    return jnp.stack([diag, lookup(rel1)], axis=1)


def _wo_router_kernel(a_ref, x_ref, wo_ref, g_ref, r_ref, x1_ref, route_ref):
    x1 = x_ref[...] + jnp.dot(a_ref[...], wo_ref[...], preferred_element_type=F32)
    x1_ref[...] = x1
    h = _rms(x1, g_ref[...])
    logits = jnp.dot(h, r_ref[...], preferred_element_type=F32,
                     precision=lax.Precision.HIGHEST)
    lane = lax.broadcasted_iota(jnp.int32, logits.shape, 1).astype(F32)
    neg = jnp.float32(-jnp.inf)
    logits = jnp.where(lane < N_EXPERTS, logits, neg)
    m1 = jnp.max(logits, axis=-1, keepdims=True)
    i1 = jnp.min(jnp.where(logits == m1, lane, float(LANES)), axis=-1, keepdims=True)
    rest = jnp.where(lane == i1, neg, logits)
    m2 = jnp.max(rest, axis=-1, keepdims=True)
    i2 = jnp.min(jnp.where(rest == m2, lane, float(LANES)), axis=-1, keepdims=True)
    e = jnp.exp(m2 - m1)
    g1 = 1.0 / (1.0 + e)
    g2 = e / (1.0 + e)
    route = jnp.where(lane == 0, i1,
                      jnp.where(lane == 1, i2,
                                jnp.where(lane == 2, g1, jnp.where(lane == 3, g2, 0.0))))
    route_ref[...] = route


def _wo_router(attn, x, wo, g, router_pad):
    t, d = x.shape
    tm = min(ROW_TILE, t)
    return pl.pallas_call(
        _wo_router_kernel,
        grid=(t // tm,),
        in_specs=[pl.BlockSpec((tm, d), lambda i: (i, 0)),
                  pl.BlockSpec((tm, d), lambda i: (i, 0)),
                  _const_spec((d, d)),
                  _const_spec((1, d)),
                  _const_spec((d, LANES))],
        out_specs=[pl.BlockSpec((tm, d), lambda i: (i, 0)),
                   pl.BlockSpec((tm, LANES), lambda i: (i, 0))],
        out_shape=[jax.ShapeDtypeStruct((t, d), F32),
                   jax.ShapeDtypeStruct((t, LANES), F32)],
        compiler_params=_params("parallel"),
        name="wo_router",
    )(attn, x, wo, g, router_pad)


def _ffn_kernel(x_ref, g_ref, wgu_ref, wd_ref, o_ref, *, d_ff, chunk):
    x = x_ref[...]
    h = _rms(x, g_ref[...]).astype(BF16)
    o_ref[...] = x + _swiglu_acc(h, wgu_ref, wd_ref, d_ff, chunk)


def _ffn_chunk(d_ff):
    return d_ff // 2 if (d_ff // 2) % LANES == 0 else d_ff


def _dense_ffn(x, g, wgu, wd):
    t, d = x.shape
    d_ff = wd.shape[0]
    tm = min(ROW_TILE, t)
    kern = functools.partial(_ffn_kernel, d_ff=d_ff, chunk=_ffn_chunk(d_ff))
    return pl.pallas_call(
        kern,
        grid=(t // tm,),
        in_specs=[pl.BlockSpec((tm, d), lambda i: (i, 0)),
                  _const_spec((1, d)),
                  _const_spec((d, 2 * d_ff)),
                  _const_spec((d_ff, d))],
        out_specs=pl.BlockSpec((tm, d), lambda i: (i, 0)),
        out_shape=jax.ShapeDtypeStruct((t, d), F32),
        compiler_params=_params("parallel"),
        name="dense_ffn",
    )(x, g, wgu, wd)


def _moe_kernel(te_ref, tv_ref, src_ref, dst_ref, gate_ref, x_hbm, g_ref, wgu_ref, wd_ref,
                out_hbm, xbuf, ybuf, sem, *, tm, d_ff, chunk):
    i = pl.program_id(0)

    @pl.when(tv_ref[i] > 0)
    def _():
        def gather(r, carry):
            t = src_ref[0, 0, r]
            pltpu.make_async_copy(x_hbm.at[pl.ds(t, 1), :], xbuf.at[pl.ds(r, 1), :],
                                  sem.at[0]).start()
            return carry
        lax.fori_loop(0, tm, gather, 0)
        pltpu.make_async_copy(x_hbm.at[pl.ds(0, tm), :], xbuf, sem.at[0]).wait()

        h = _rms(xbuf[...], g_ref[...]).astype(BF16)
        ybuf[...] = _swiglu_acc(h, wgu_ref, wd_ref, d_ff, chunk) * gate_ref[...]

    @pl.when(tv_ref[i] == 0)
    def _():
        ybuf[...] = jnp.zeros(ybuf.shape, F32)

    def scatter(r, carry):
        t = dst_ref[0, 0, r]
        pltpu.make_async_copy(ybuf.at[pl.ds(r, 1), :], out_hbm.at[pl.ds(t, 1), :],
                              sem.at[1]).start()
        return carry
    lax.fori_loop(0, tm, scatter, 0)
    pltpu.make_async_copy(ybuf, out_hbm.at[pl.ds(0, tm), :], sem.at[1]).wait()


def _moe_ffn(x1, route, g, wgu, wd):
    t, d = x1.shape
    d_ff = wd.shape[1]
    tm = min(MOE_TILE, t)
    n_pairs = TOP_K * t
    n_tiles = n_pairs // tm + N_EXPERTS
    n_rows = n_tiles * tm

    idx = route[:, :TOP_K].astype(jnp.int32).reshape(-1)
    gates = route[:, TOP_K:2 * TOP_K].reshape(-1)
    onehot = (idx[:, None] == jnp.arange(N_EXPERTS, dtype=jnp.int32)[None, :]).astype(jnp.int32)
    rank = jnp.take_along_axis(jnp.cumsum(onehot, axis=0), idx[:, None], axis=1)[:, 0] - 1
    counts = jnp.sum(onehot, axis=0)
    tiles_per = (counts + tm - 1) // tm
    tile_end = jnp.cumsum(tiles_per)
    seg_start = (tile_end - tiles_per) * tm
    pos = seg_start[idx] + rank
    row_pair = jnp.full((n_rows,), -1, jnp.int32).at[pos].set(
        jnp.arange(n_pairs, dtype=jnp.int32))
    valid = (row_pair >= 0).astype(jnp.int32)
    pair = jnp.maximum(row_pair, 0)
    spare = n_pairs + jnp.cumsum(1 - valid) - 1
    row_src = jnp.where(valid == 1, pair // TOP_K, 0)
    row_dst = jnp.where(valid == 1, (pair % TOP_K) * t + pair // TOP_K, spare)
    row_gate = jnp.where(valid == 1, gates[pair], 0.0)
    tile_ids = jnp.arange(n_tiles, dtype=jnp.int32)
    tile_expert = jnp.minimum(jnp.searchsorted(tile_end, tile_ids, side="right"),
                              N_EXPERTS - 1).astype(jnp.int32)
    n_used = tile_end[-1]
    tile_valid = (tile_ids < n_used).astype(jnp.int32)
    last_expert = tile_expert[jnp.maximum(n_used - 1, 0)]
    tile_expert = jnp.where(tile_valid == 1, tile_expert, last_expert)

    kern = functools.partial(_moe_kernel, tm=tm, d_ff=d_ff, chunk=_ffn_chunk(d_ff))
    grid_spec = pltpu.PrefetchScalarGridSpec(
        num_scalar_prefetch=2,
        grid=(n_tiles,),
        in_specs=[
            pl.BlockSpec((1, 1, tm), lambda i, te, tv: (i, 0, 0), memory_space=pltpu.SMEM),
            pl.BlockSpec((1, 1, tm), lambda i, te, tv: (i, 0, 0), memory_space=pltpu.SMEM),
            pl.BlockSpec((tm, 1), lambda i, te, tv: (i, 0)),
            pl.BlockSpec(memory_space=pl.ANY),
            pl.BlockSpec((1, d), lambda i, te, tv: (0, 0)),
            pl.BlockSpec((None, d, 2 * d_ff), lambda i, te, tv: (te[i], 0, 0)),
            pl.BlockSpec((None, d_ff, d), lambda i, te, tv: (te[i], 0, 0)),
        ],
        out_specs=pl.BlockSpec(memory_space=pl.ANY),
        scratch_shapes=[pltpu.VMEM((tm, d), F32), pltpu.VMEM((tm, d), F32),
                        pltpu.SemaphoreType.DMA((2,))],
    )
    return pl.pallas_call(
        kern,
        grid_spec=grid_spec,
        out_shape=jax.ShapeDtypeStruct((n_rows, d), F32),
        compiler_params=_params("arbitrary"),
        name="moe_ffn",
    )(tile_expert, tile_valid, row_src.reshape(n_tiles, 1, tm), row_dst.reshape(n_tiles, 1, tm),
      row_gate.reshape(n_rows, 1), x1, g, wgu, wd)


def _combine_kernel(x_ref, y0_ref, y1_ref, g_ref, o_ref, *, final_norm):
    x = x_ref[...] + y0_ref[...] + y1_ref[...]
    o_ref[...] = _rms(x, g_ref[...]) if final_norm else x


def _moe_combine(x1, out2, g, final_norm):
    t, d = x1.shape
    tm = min(ROW_TILE, t)
    nb = t // tm
    kern = functools.partial(_combine_kernel, final_norm=final_norm)
    return pl.pallas_call(
        kern,
        grid=(nb,),
        in_specs=[pl.BlockSpec((tm, d), lambda i: (i, 0)),
                  pl.BlockSpec((tm, d), lambda i: (i, 0)),
                  pl.BlockSpec((tm, d), lambda i: (nb + i, 0)),
                  _const_spec((1, d))],
        out_specs=pl.BlockSpec((tm, d), lambda i: (i, 0)),
        out_shape=jax.ShapeDtypeStruct((t, d), F32),
        compiler_params=_params("parallel"),
        name="moe_combine",
    )(x1, out2, out2, g)


def _ssm_kernel(x_ref, g_ref, bd_ref, cd_ref, a_ref, pw_ref, d_ref, wglu_ref, o_ref,
                xp_ref, xc_ref, xs_ref, init_ref, carry_ref, *, tt, n_kb):
    seg = tt // SUBLANES
    half = xs_ref.shape[1] // (2 * n_kb)
    d = x_ref.shape[1]
    kb_in = d // n_kb

    @pl.when(pl.program_id(1) == 0)
    def _():
        carry_ref[...] = jnp.zeros(carry_ref.shape, F32)

    n_lc = d // LANES
    for c in range(n_lc):
        xc_ref[c] = x_ref[:, c * LANES:(c + 1) * LANES]
    for i in range(seg):
        for c in range(n_lc):
            xp_ref[pl.ds(SUBLANES * i, SUBLANES), c * LANES:(c + 1) * LANES] = (
                xc_ref[c, pl.ds(i, SUBLANES, stride=seg), :])

    xp = xp_ref[...]
    u = _rms(xp, g_ref[...])
    ub = u.astype(BF16)

    y_parts = []
    for kb in range(n_kb):
        c0 = 2 * half * kb
        xs_ref[:, c0:c0 + 2 * half] = jnp.dot(ub[:, kb * kb_in:(kb + 1) * kb_in], bd_ref[kb],
                                              preferred_element_type=F32)
        a_re = jnp.broadcast_to(a_ref[0:1, kb * half:(kb + 1) * half], (SUBLANES, half))
        a_im = jnp.broadcast_to(a_ref[1:2, kb * half:(kb + 1) * half], (SUBLANES, half))

        def scan_body(i, s, c0=c0, a_re=a_re, a_im=a_im):
            s_re, s_im = s
            r0 = pl.multiple_of(i * SUBLANES, SUBLANES)
            x_re = xs_ref[pl.ds(r0, SUBLANES), c0:c0 + half]
            x_im = xs_ref[pl.ds(r0, SUBLANES), c0 + half:c0 + 2 * half]
            n_re = a_re * s_re - a_im * s_im + x_re
            n_im = a_re * s_im + a_im * s_re + x_im
            xs_ref[pl.ds(r0, SUBLANES), c0:c0 + half] = n_re
            xs_ref[pl.ds(r0, SUBLANES), c0 + half:c0 + 2 * half] = n_im
            return n_re, n_im

        zero = jnp.zeros((SUBLANES, half), F32)
        e_re, e_im = lax.fori_loop(0, seg, scan_body, (zero, zero))

        al_re = a_ref[2:3, kb * half:(kb + 1) * half]
        al_im = a_ref[3:4, kb * half:(kb + 1) * half]
        in_re = carry_ref[0:1, kb * half:(kb + 1) * half]
        in_im = carry_ref[1:2, kb * half:(kb + 1) * half]
        for j in range(SUBLANES):
            init_ref[j:j + 1, 0:half] = in_re
            init_ref[j:j + 1, half:2 * half] = in_im
            n_re = e_re[j:j + 1] + al_re * in_re - al_im * in_im
            n_im = e_im[j:j + 1] + al_re * in_im + al_im * in_re
            in_re, in_im = n_re, n_im
        carry_ref[0:1, kb * half:(kb + 1) * half] = in_re
        carry_ref[1:2, kb * half:(kb + 1) * half] = in_im
        i_re = init_ref[:, 0:half]
        i_im = init_ref[:, half:2 * half]

        def fix_body(i, carry, c0=c0, i_re=i_re, i_im=i_im, kb=kb):
            r0 = pl.multiple_of(i * SUBLANES, SUBLANES)
            p_re = pw_ref[0, pl.ds(i, 1), kb * half:(kb + 1) * half]
            p_im = pw_ref[1, pl.ds(i, 1), kb * half:(kb + 1) * half]
            s_re = xs_ref[pl.ds(r0, SUBLANES), c0:c0 + half]
            s_im = xs_ref[pl.ds(r0, SUBLANES), c0 + half:c0 + 2 * half]
            xs_ref[pl.ds(r0, SUBLANES), c0:c0 + half] = s_re + p_re * i_re - p_im * i_im
            xs_ref[pl.ds(r0, SUBLANES), c0 + half:c0 + 2 * half] = s_im + p_re * i_im + p_im * i_re
            return carry

        lax.fori_loop(0, seg, fix_body, 0)
        y_parts.append(jnp.dot(xs_ref[:, c0:c0 + 2 * half].astype(BF16), cd_ref[kb],
                               preferred_element_type=F32))

    y = jnp.concatenate(y_parts, axis=1) + d_ref[...] * u
    z = jnp.dot(jax.nn.gelu(y).astype(BF16), wglu_ref[...], preferred_element_type=F32)
    res = xp + z[:, :d] * jax.nn.sigmoid(z[:, d:])
    xp_ref[...] = res
    for i in range(seg):
        for c in range(n_lc):
            xc_ref[c, pl.ds(i, SUBLANES, stride=seg), :] = (
                xp_ref[pl.ds(SUBLANES * i, SUBLANES), c * LANES:(c + 1) * LANES])
    for c in range(n_lc):
        o_ref[:, c * LANES:(c + 1) * LANES] = xc_ref[c]


def _ssm_tables(a_re, a_im, log_dt, b_re, b_im, c_re, c_im, seg, n_kb):
    g_, p_ = a_re.shape
    c_ = b_re.shape[-1]
    gk = g_ // n_kb
    dt = jnp.exp(log_dt.astype(F32))[:, None]
    ar = a_re.astype(F32)
    ai = a_im.astype(F32)
    mag = jnp.exp(ar * dt)
    ab_re = mag * jnp.cos(ai * dt)
    ab_im = mag * jnp.sin(ai * dt)
    den = ar * ar + ai * ai
    nr = ab_re - 1.0
    ni = ab_im
    coef_re = (nr * ar + ni * ai) / den
    coef_im = (ni * ar - nr * ai) / den
    br = b_re.astype(F32)
    bi = b_im.astype(F32)
    bb_re = coef_re[..., None] * br - coef_im[..., None] * bi
    bb_im = coef_re[..., None] * bi + coef_im[..., None] * br
    eye = jnp.eye(gk, dtype=F32)

    def blockdiag_in(w):
        w = w.reshape(n_kb, gk, p_, c_)
        return jnp.einsum("kgpc,gh->kgchp", w, eye).reshape(n_kb, gk * c_, gk * p_)

    def blockdiag_out(w):
        w = w.reshape(n_kb, gk, c_, p_)
        return jnp.einsum("kgcp,gh->kgphc", w, eye).reshape(n_kb, gk * p_, gk * c_)

    bd = jnp.concatenate([blockdiag_in(bb_re), blockdiag_in(bb_im)], axis=2).astype(BF16)
    cd = jnp.concatenate([blockdiag_out(c_re.astype(F32)),
                          -blockdiag_out(c_im.astype(F32))], axis=1).astype(BF16)

    def cpow(n):
        m = jnp.exp(ar * dt * n)
        return m * jnp.cos(ai * dt * n), m * jnp.sin(ai * dt * n)

    al_re, al_im = cpow(float(seg))
    a_tab = jnp.stack([ab_re.reshape(-1), ab_im.reshape(-1),
                       al_re.reshape(-1), al_im.reshape(-1)], axis=0)
    n = jnp.arange(1, seg + 1, dtype=F32)[:, None, None]
    pm = jnp.exp(ar[None] * dt[None] * n)
    pw = jnp.stack([(pm * jnp.cos(ai[None] * dt[None] * n)).reshape(seg, -1),
                    (pm * jnp.sin(ai[None] * dt[None] * n)).reshape(seg, -1)], axis=0)
    return bd, cd, a_tab, pw


def _ssm_mixer(x, g, tables, d_skip, wglu, batch):
    t, d = x.shape
    seq = t // batch
    tt = min(SSM_TILE, seq)
    nt = seq // tt
    bd, cd, a_tab, pw = tables
    n_kb = bd.shape[0]
    n_state = a_tab.shape[1]
    kern = functools.partial(_ssm_kernel, tt=tt, n_kb=n_kb)
    return pl.pallas_call(
        kern,
        grid=(batch, nt),
        in_specs=[pl.BlockSpec((tt, d), lambda b, n: (b * nt + n, 0)),
                  _const_spec((1, d)),
                  _const_spec(bd.shape),
                  _const_spec(cd.shape),
                  _const_spec(a_tab.shape),
                  _const_spec(pw.shape),
                  _const_spec((1, d)),
                  _const_spec(wglu.shape)],
        out_specs=pl.BlockSpec((tt, d), lambda b, n: (b * nt + n, 0)),
        out_shape=jax.ShapeDtypeStruct((t, d), F32),
        scratch_shapes=[pltpu.VMEM((tt, d), F32),
                        pltpu.VMEM((d // LANES, tt, LANES), F32),
                        pltpu.VMEM((tt, 2 * n_state), F32),
                        pltpu.VMEM((SUBLANES, 2 * n_state // n_kb), F32),
                        pltpu.VMEM((2, n_state), F32)],
        compiler_params=_params("parallel", "arbitrary"),
        name="ssm_mixer",
    )(x, g, bd, cd, a_tab, pw, d_skip, wglu)


def kernel(x, norm_mix_g, norm_ffn_g, final_norm_g, ssm_a_re, ssm_a_im, ssm_log_dt, ssm_b_re,
           ssm_b_im, ssm_c_re, ssm_c_im, ssm_d, ssm_w_glu, attn_w_qkv, attn_lambda_q1,
           attn_lambda_k1, attn_lambda_q2, attn_lambda_k2, attn_subln_g, attn_w_o, rel_bias,
           ffn_w_gate_up, ffn_w_down, moe_router, moe_w_gate_up, moe_w_down):
    batch, seq, d = x.shape
    depth = norm_mix_g.shape[0]
    xf = x.reshape(batch * seq, d).astype(F32)
    row = lambda v: v.reshape(1, -1).astype(F32)

    ssm_tt = min(SSM_TILE, seq)
    n_kb = d // MXU_DIM
    blk = min(ATTN_BLOCK, seq)
    bias_tiles = _bias_tiles(rel_bias, blk)

    for i in range(depth):
        j = i // 2
        if i % 2 == 0:
            tables = _ssm_tables(ssm_a_re[j], ssm_a_im[j], ssm_log_dt[j], ssm_b_re[j],
                                 ssm_b_im[j], ssm_c_re[j], ssm_c_im[j],
                                 ssm_tt // SUBLANES, n_kb)
            xf = _ssm_mixer(xf, row(norm_mix_g[i]), tables, row(ssm_d[j]),
                            ssm_w_glu[j].astype(BF16), batch)
            xf = _dense_ffn(xf, row(norm_ffn_g[i]), ffn_w_gate_up[j].astype(BF16),
                            ffn_w_down[j].astype(BF16))
        else:
            lambda_init = 0.8 - 0.6 * math.exp(-0.3 * i)
            qkv = _qkv_proj(xf, row(norm_mix_g[i]), attn_w_qkv[j].astype(BF16))
            lqk = jnp.stack([attn_lambda_q1[j], attn_lambda_k1[j],
                             attn_lambda_q2[j], attn_lambda_k2[j]], axis=0).astype(F32)
            attn = _attention(qkv, bias_tiles, lqk, row(attn_subln_g[j]), batch, lambda_init)
            router_pad = jnp.pad(moe_router[j].astype(F32), ((0, 0), (0, LANES - N_EXPERTS)))
            x1, route = _wo_router(attn, xf, attn_w_o[j].astype(BF16), row(norm_ffn_g[i]),
                                   router_pad)
            out2 = _moe_ffn(x1, route, row(norm_ffn_g[i]), moe_w_gate_up[j].astype(BF16),
                            moe_w_down[j].astype(BF16))
            last = i == depth - 1
            xf = _moe_combine(x1, out2, row(final_norm_g), final_norm=last)
    if depth % 2 == 1:
        raise NotImplementedError("trunk depth must be even (final norm is fused into the MoE combine)")
    return xf.reshape(batch, seq, d).astype(x.dtype)
```
